```python
import jax, jax.numpy as jnp
from jax import lax
import numpy as np

D_MODEL = 1024
BATCH = 8
SEQ = 4096
DEPTH = 1

MIX_WIDTH = 2 * D_MODEL
HEAD_DIM = 128
GDN_HEADS = 6
MLSTM_HEADS = 6
MEM_HEADS = 4
GDN_W = GDN_HEADS * HEAD_DIM
MLSTM_W = MLSTM_HEADS * HEAD_DIM
MEM_W = MEM_HEADS * HEAD_DIM
N_MEM = 256
CONV_WIDTH = 4
CHUNK = 64
EPS = 1e-6
SPLITS = [GDN_W, GDN_W, GDN_W, GDN_HEADS, GDN_HEADS, GDN_W,
          MLSTM_W, MLSTM_W, MLSTM_W, MLSTM_HEADS, MLSTM_HEADS, MLSTM_W, MLSTM_W,
          MEM_W, MEM_W]
IN_COLS = 4 * GDN_W + 2 * GDN_HEADS + 5 * MLSTM_W + 2 * MLSTM_HEADS + 2 * MEM_W

kernel_name = 'hybrid_gdn_mlstm_memxattn_block'


def rms_norm(x, g):
    xf = x.astype(jnp.float32)
    y = xf * lax.rsqrt(jnp.mean(xf * xf, axis=-1, keepdims=True) + EPS)
    return y * g.astype(jnp.float32)


def l2_normalize(x):
    return x * lax.rsqrt(jnp.sum(x * x, axis=-1, keepdims=True) + EPS)


def causal_depthwise_conv(x, w):
    k = w.shape[0]
    return lax.conv_general_dilated(x, w[:, None, :], window_strides=(1,), padding=[(k - 1, 0)],
                                    dimension_numbers=('NWC', 'WIO', 'NWC'),
                                    feature_group_count=x.shape[-1])


def to_chunks(x):
    b, t, h, d = x.shape
    return x.reshape(b, t // CHUNK, CHUNK, h, d).transpose(0, 3, 1, 2, 4)


def gate_to_chunks(x):
    b, t, h = x.shape
    return x.reshape(b, t // CHUNK, CHUNK, h).transpose(0, 3, 1, 2)


def from_chunks(x):
    b, h, nc, l, d = x.shape
    return x.transpose(0, 2, 3, 1, 4).reshape(b, nc * l, h, d)


def gated_delta_rule(q, k, v, g, beta):
    b_, h_, nc, l, d = q.shape
    causal = jnp.tril(jnp.ones((l, l), dtype=bool))
    strict = jnp.tril(jnp.ones((l, l), dtype=bool), -1)
    gc = jnp.cumsum(g, axis=-1)
    g_last = gc[..., -1]
    decay = jnp.exp(jnp.where(causal, gc[..., :, None] - gc[..., None, :], -jnp.inf))
    k_beta = k * beta[..., None]
    a_mat = jnp.where(strict, jnp.einsum('bhcid,bhcjd->bhcij', k_beta, k) * decay, 0.0)
    t_mat = a_mat + jnp.eye(l, dtype=q.dtype)
    u = lax.linalg.triangular_solve(t_mat, v * beta[..., None], left_side=True, lower=True,
                                    unit_diagonal=True)
    w = lax.linalg.triangular_solve(t_mat, k_beta * jnp.exp(gc)[..., None], left_side=True,
                                    lower=True, unit_diagonal=True)
    attn = jnp.einsum('bhcid,bhcjd->bhcij', q, k) * decay
    q_dec = q * jnp.exp(gc)[..., None]
    k_dec = k * jnp.exp(g_last[..., None] - gc)[..., None]
    xs = tuple(jnp.moveaxis(t, 2, 0) for t in (q_dec, w, u, attn, k_dec, jnp.exp(g_last)))

    def step(s, inp):
        q_c, w_c, u_c, a_c, k_c, gl_c = inp
        v_new = u_c - jnp.einsum('bhld,bhde->bhle', w_c, s)
        o = jnp.einsum('bhld,bhde->bhle', q_c, s) + jnp.einsum('bhlm,bhme->bhle', a_c, v_new)
        s = s * gl_c[..., None, None] + jnp.einsum('bhld,bhle->bhde', k_c, v_new)
        return s, o

    s0 = jnp.zeros((b_, h_, d, v.shape[-1]), q.dtype)
    _, o = lax.scan(step, s0, xs)
    return jnp.moveaxis(o, 0, 2)


def mlstm_chunkwise(q, k, v, log_i, log_f):
    b_, h_, nc, l, d = q.shape
    causal = jnp.tril(jnp.ones((l, l), dtype=bool))
    bcum = jnp.cumsum(log_f, axis=-1)
    b_last = bcum[..., -1]
    log_w = jnp.where(causal, bcum[..., :, None] - bcum[..., None, :] + log_i[..., None, :], -jnp.inf)
    m_intra = jnp.max(log_w, axis=-1)
    p = jnp.exp(log_w - m_intra[..., None]) * jnp.einsum('bhctd,bhcsd->bhcts', q, k)
    num_intra = jnp.einsum('bhcts,bhcsd->bhctd', p, v)
    den_intra = jnp.sum(p, axis=-1)
    log_end = b_last[..., None] - bcum + log_i
    m_loc = jnp.max(log_end, axis=-1)
    k_end = k * jnp.exp(log_end - m_loc[..., None])[..., None]
    xs = tuple(jnp.moveaxis(t, 2, 0) for t in
               (q, k_end, v, bcum, b_last, m_intra, num_intra, den_intra, m_loc))

    def step(carry, inp):
        c_st, n_st, m_st = carry
        q_c, k_c, v_c, b_c, bl_c, mi_c, ni_c, di_c, ml_c = inp
        m_t = jnp.maximum(b_c + m_st[..., None], mi_c)
        a = jnp.exp(b_c + m_st[..., None] - m_t)
        r = jnp.exp(mi_c - m_t)
        num = a[..., None] * jnp.einsum('bhld,bhde->bhle', q_c, c_st) + r[..., None] * ni_c
        den = a * jnp.einsum('bhld,bhd->bhl', q_c, n_st) + r * di_c
        h = num / jnp.maximum(jnp.abs(den), jnp.exp(-m_t))[..., None]
        m_new = jnp.maximum(bl_c + m_st, ml_c)
        s_old = jnp.exp(bl_c + m_st - m_new)
        s_loc = jnp.exp(ml_c - m_new)
        c_st = s_old[..., None, None] * c_st + s_loc[..., None, None] * jnp.einsum('bhld,bhle->bhde', k_c, v_c)
        n_st = s_old[..., None] * n_st + s_loc[..., None] * jnp.sum(k_c, axis=-2)
        return (c_st, n_st, m_new), h

    init = (jnp.zeros((b_, h_, d, d), q.dtype), jnp.zeros((b_, h_, d), q.dtype),
            jnp.zeros((b_, h_), q.dtype))
    _, h = lax.scan(step, init, xs)
    return jnp.moveaxis(h, 0, 2)


def setup_inputs(seed: int = 0) -> dict:
    key = jax.random.key(seed)
    ks = jax.random.split(key, 16)
    f32 = jnp.float32
    nrm = jax.random.normal
    x = nrm(ks[0], (BATCH, SEQ, D_MODEL), f32)
    mem = nrm(ks[1], (BATCH, N_MEM, D_MODEL), f32)
    norm_g = 1.0 + 0.02 * nrm(ks[2], (DEPTH, D_MODEL), f32)
    w_in = nrm(ks[3], (DEPTH, D_MODEL, IN_COLS), f32) * D_MODEL ** -0.5
    conv_w = nrm(ks[4], (DEPTH, CONV_WIDTH, 3 * GDN_W), f32) * CONV_WIDTH ** -0.5
    gdn_a_log = jnp.log(jax.random.uniform(ks[5], (DEPTH, GDN_HEADS), f32, minval=1.0, maxval=16.0))
    dt = jnp.exp(jax.random.uniform(ks[6], (DEPTH, GDN_HEADS), f32,
                                    minval=math_log(1e-3), maxval=math_log(1e-1)))
    gdn_dt_bias = dt + jnp.log(-jnp.expm1(-dt))
    gdn_norm_g = 1.0 + 0.02 * nrm(ks[7], (DEPTH, HEAD_DIM), f32)
    mlstm_i_bias = 0.1 * nrm(ks[8], (DEPTH, MLSTM_HEADS), f32)
    mlstm_f_bias = jax.random.uniform(ks[9], (DEPTH, MLSTM_HEADS), f32, minval=3.0, maxval=6.0)
    mlstm_norm_g = 1.0 + 0.02 * nrm(ks[10], (DEPTH, MLSTM_HEADS, HEAD_DIM), f32)
    mem_norm_g = 1.0 + 0.02 * nrm(ks[11], (DEPTH, D_MODEL), f32)
    w_mem_kv = nrm(ks[12], (DEPTH, D_MODEL, 2 * MEM_W), f32) * D_MODEL ** -0.5
    w_out = nrm(ks[13], (DEPTH, MIX_WIDTH, D_MODEL), f32) * MIX_WIDTH ** -0.5
    final_norm_g = 1.0 + 0.02 * nrm(ks[14], (D_MODEL,), f32)
    return {'x': x, 'mem': mem, 'norm_g': norm_g, 'w_in': w_in, 'conv_w': conv_w,
            'gdn_a_log': gdn_a_log, 'gdn_dt_bias': gdn_dt_bias, 'gdn_norm_g': gdn_norm_g,
            'mlstm_i_bias': mlstm_i_bias, 'mlstm_f_bias': mlstm_f_bias, 'mlstm_norm_g': mlstm_norm_g,
            'mem_norm_g': mem_norm_g, 'w_mem_kv': w_mem_kv, 'w_out': w_out,
            'final_norm_g': final_norm_g}


def math_log(v):
    return float(np.log(v))


def reference(x, mem, norm_g, w_in, conv_w, gdn_a_log, gdn_dt_bias, gdn_norm_g,
              mlstm_i_bias, mlstm_f_bias, mlstm_norm_g, mem_norm_g, w_mem_kv, w_out,
              final_norm_g):
    f32 = jnp.float32
    bsz, seq, _ = x.shape
    split_idx = [int(c) for c in np.cumsum(SPLITS)[:-1]]
    h_res = x.astype(f32)
    for layer in range(DEPTH):
        hn = rms_norm(h_res, norm_g[layer])
        proj = hn @ w_in[layer].astype(f32)
        (g_q, g_k, g_v, g_a, g_b, g_z,
         m_q, m_k, m_v, m_i, m_f, m_o, m_z,
         x_q, x_z) = jnp.split(proj, split_idx, axis=-1)

        qkv = jax.nn.silu(causal_depthwise_conv(jnp.concatenate([g_q, g_k, g_v], axis=-1),
                                                conv_w[layer].astype(f32)))
        cq, ck, cv = jnp.split(qkv, 3, axis=-1)
        cq = l2_normalize(cq.reshape(bsz, seq, GDN_HEADS, HEAD_DIM)) * HEAD_DIM ** -0.5
        ck = l2_normalize(ck.reshape(bsz, seq, GDN_HEADS, HEAD_DIM))
        cv = cv.reshape(bsz, seq, GDN_HEADS, HEAD_DIM)
        g_log = -jnp.exp(gdn_a_log[layer].astype(f32)) * jax.nn.softplus(g_a + gdn_dt_bias[layer].astype(f32))
        beta = jax.nn.sigmoid(g_b)
        o_gdn = gated_delta_rule(to_chunks(cq), to_chunks(ck), to_chunks(cv),
                                 gate_to_chunks(g_log), gate_to_chunks(beta))
        o_gdn = rms_norm(from_chunks(o_gdn), gdn_norm_g[layer]).reshape(bsz, seq, GDN_W)
        o_gdn = o_gdn * jax.nn.silu(g_z)

        mq = m_q.reshape(bsz, seq, MLSTM_HEADS, HEAD_DIM)
        mk = m_k.reshape(bsz, seq, MLSTM_HEADS, HEAD_DIM) * HEAD_DIM ** -0.5
        mv = m_v.reshape(bsz, seq, MLSTM_HEADS, HEAD_DIM)
        log_i = m_i + mlstm_i_bias[layer].astype(f32)
        log_f = jax.nn.log_sigmoid(m_f + mlstm_f_bias[layer].astype(f32))
        h_ml = mlstm_chunkwise(to_chunks(mq), to_chunks(mk), to_chunks(mv),
                               gate_to_chunks(log_i), gate_to_chunks(log_f))
        h_ml = from_chunks(h_ml) * jax.nn.sigmoid(m_o).reshape(bsz, seq, MLSTM_HEADS, HEAD_DIM)
        h_ml = rms_norm(h_ml, mlstm_norm_g[layer]).reshape(bsz, seq, MLSTM_W)
        h_ml = h_ml * jax.nn.silu(m_z)

        mem_n = rms_norm(mem, mem_norm_g[layer])
        kv = mem_n @ w_mem_kv[layer].astype(f32)
        mem_k, mem_v = jnp.split(kv, 2, axis=-1)
        mem_k = mem_k.reshape(bsz, -1, MEM_HEADS, HEAD_DIM)
        mem_v = mem_v.reshape(bsz, -1, MEM_HEADS, HEAD_DIM)
        xq = x_q.reshape(bsz, seq, MEM_HEADS, HEAD_DIM)
        scores = jnp.einsum('bthd,bmhd->bhtm', xq, mem_k) * HEAD_DIM ** -0.5
        probs = jax.nn.softmax(scores, axis=-1)
        o_mem = jnp.einsum('bhtm,bmhd->bthd', probs, mem_v).reshape(bsz, seq, MEM_W)
        o_mem = o_mem * jax.nn.silu(x_z)

        mixed = jnp.concatenate([o_gdn, h_ml, o_mem], axis=-1)
        h_res = h_res + mixed @ w_out[layer].astype(f32)
    return rms_norm(h_res, final_norm_g).astype(x.dtype)
```

```python
import functools

import jax
import jax.numpy as jnp
from jax import lax
from jax.experimental import pallas as pl
from jax.experimental.pallas import tpu as pltpu

F32 = jnp.float32
BF16 = jnp.bfloat16

HEAD_DIM = 128
GDN_HEADS = 6
MLSTM_HEADS = 6
MEM_HEADS = 4
GDN_W = GDN_HEADS * HEAD_DIM
MLSTM_W = MLSTM_HEADS * HEAD_DIM
MEM_W = MEM_HEADS * HEAD_DIM
CONV_WIDTH = 4
CHUNK = 64
EPS = 1e-6
LANES = 128
SUBLANES = 8
VMEM_LIMIT = 56 * 1024 * 1024

GATE_GA = 0
GATE_GB = GDN_HEADS
GATE_MI = 2 * GDN_HEADS
GATE_MF = 2 * GDN_HEADS + MLSTM_HEADS


def _dot(a, b):
    return jnp.dot(a.astype(BF16), b.astype(BF16), preferred_element_type=F32)


def _dot_nt(a, b):
    return lax.dot_general(a.astype(BF16), b.astype(BF16), (((1,), (1,)), ((), ())),
                           preferred_element_type=F32)


def _dot_tn(a, b):
    return lax.dot_general(a.astype(BF16), b.astype(BF16), (((0,), (0,)), ((), ())),
                           preferred_element_type=F32)


def _split2(x):
    hi = x.astype(BF16)
    lo = (x - hi.astype(F32)).astype(BF16)
    return hi, lo


def _sigmoid(x):
    return 1.0 / (1.0 + jnp.exp(-x))


def _silu(x):
    return x * _sigmoid(x)


def _softplus(x):
    return jnp.maximum(x, 0.0) + jnp.log1p(jnp.exp(-jnp.abs(x)))


def _chunk_tri(n):
    r = lax.broadcasted_iota(jnp.int32, (n, n), 0)
    c = lax.broadcasted_iota(jnp.int32, (n, n), 1)
    return jnp.where((r // CHUNK == c // CHUNK) & (c <= r), 1.0, 0.0).astype(BF16)


def _chunk_cumsum(tri, x):
    hi, lo = _split2(x)
    return (jnp.dot(tri, hi, preferred_element_type=F32)
            + jnp.dot(tri, lo, preferred_element_type=F32))


def _inproj_kernel(x_ref, ng_ref, wg_ref, wm_ref, wx_ref, wgate_ref,
                   pg_ref, pm_ref, px_ref, gates_ref):
    x = x_ref[...].astype(F32)
    hn = x * lax.rsqrt(jnp.mean(x * x, axis=-1, keepdims=True) + EPS) * ng_ref[...]
    hb = hn.astype(BF16)
    pg_ref[...] = jnp.dot(hb, wg_ref[...], preferred_element_type=F32).astype(pg_ref.dtype)
    pm_ref[...] = jnp.dot(hb, wm_ref[...], preferred_element_type=F32).astype(pm_ref.dtype)
    px_ref[...] = jnp.dot(hb, wx_ref[...], preferred_element_type=F32).astype(px_ref.dtype)
    h_lo = (hn - hb.astype(F32)).astype(BF16)
    w_hi, w_lo = _split2(wgate_ref[...])
    gates_ref[...] = (jnp.dot(hb, w_hi, preferred_element_type=F32)
                      + jnp.dot(hb, w_lo, preferred_element_type=F32)
                      + jnp.dot(h_lo, w_hi, preferred_element_type=F32))


def _inproj(x2, norm_g, wg, wm, wx, wgate, tm):
    m, d = x2.shape
    full = lambda a: pl.BlockSpec(a.shape, lambda i: (0, 0))
    return pl.pallas_call(
        _inproj_kernel,
        grid=(m // tm,),
        in_specs=[pl.BlockSpec((tm, d), lambda i: (i, 0)), full(norm_g), full(wg), full(wm), full(wx),
                  full(wgate)],
        out_specs=[pl.BlockSpec((tm, wg.shape[1]), lambda i: (i, 0)),
                   pl.BlockSpec((tm, wm.shape[1]), lambda i: (i, 0)),
                   pl.BlockSpec((tm, wx.shape[1]), lambda i: (i, 0)),
                   pl.BlockSpec((tm, LANES), lambda i: (i, 0))],
        out_shape=[jax.ShapeDtypeStruct((m, wg.shape[1]), BF16),
                   jax.ShapeDtypeStruct((m, wm.shape[1]), BF16),
                   jax.ShapeDtypeStruct((m, wx.shape[1]), BF16),
                   jax.ShapeDtypeStruct((m, LANES), F32)],
        compiler_params=pltpu.CompilerParams(dimension_semantics=("arbitrary",),
                                             vmem_limit_bytes=VMEM_LIMIT),
        name="inproj",
    )(x2, norm_g, wg, wm, wx, wgate)


def _memkv_kernel(mem_ref, g_ref, w_ref, kv_ref):
    x = mem_ref[0].astype(F32)
    xn = x * lax.rsqrt(jnp.mean(x * x, axis=-1, keepdims=True) + EPS) * g_ref[...]
    kv_ref[0] = _dot(xn, w_ref[...]).astype(kv_ref.dtype)


def _memkv(mem, g, w):
    b, n, d = mem.shape
    return pl.pallas_call(
        _memkv_kernel,
        grid=(b,),
        in_specs=[pl.BlockSpec((1, n, d), lambda i: (i, 0, 0)),
                  pl.BlockSpec(g.shape, lambda i: (0, 0)),
                  pl.BlockSpec(w.shape, lambda i: (0, 0))],
        out_specs=pl.BlockSpec((1, n, w.shape[1]), lambda i: (i, 0, 0)),
        out_shape=jax.ShapeDtypeStruct((b, n, w.shape[1]), BF16),
        compiler_params=pltpu.CompilerParams(dimension_semantics=("arbitrary",),
                                             vmem_limit_bytes=VMEM_LIMIT),
        name="memkv",
    )(mem, g, w)


def _unit_lower_inverse(a, row, col):
    eye = jnp.where(row == col, 1.0, 0.0)
    x = eye - jnp.where(row // 2 == col // 2, a, 0.0)
    s = 2
    while s < CHUNK:
        a_l = jnp.where((row // (2 * s) == col // (2 * s)) & (row // s != col // s), a, 0.0)
        x = x - _dot(_dot(x, a_l), x)
        s *= 2
    return x


def _gdn_kernel(qkv_ref, z_ref, gates_ref, convw_ref, alog_ref, dtb_ref, ng_ref,
                o_ref, xpad_ref, s_ref, obuf_ref):
    tq = qkv_ref.shape[0]
    nchunk = tq // CHUNK

    @pl.when(pl.program_id(1) == 0)
    def _():
        xpad_ref[0:SUBLANES, :] = jnp.zeros((SUBLANES, xpad_ref.shape[1]), F32)
        s_ref[...] = jnp.zeros(s_ref.shape, F32)

    xpad_ref[SUBLANES:SUBLANES + tq, :] = qkv_ref[...].astype(F32)
    acc = convw_ref[CONV_WIDTH - 1:CONV_WIDTH, :] * xpad_ref[SUBLANES:SUBLANES + tq, :]
    for j in range(CONV_WIDTH - 1):
        off = SUBLANES - (CONV_WIDTH - 1) + j
        acc = acc + convw_ref[j:j + 1, :] * xpad_ref[off:off + tq, :]
    xpad_ref[0:SUBLANES, :] = xpad_ref[tq:tq + SUBLANES, :]
    y = _silu(acc)

    gates = gates_ref[...]
    g_log = -jnp.exp(alog_ref[...]) * _softplus(gates + dtb_ref[...])
    beta = _sigmoid(gates)
    gc = _chunk_cumsum(_chunk_tri(tq), g_log)

    row = lax.broadcasted_iota(jnp.int32, (CHUNK, CHUNK), 0)
    col = lax.broadcasted_iota(jnp.int32, (CHUNK, CHUNK), 1)
    causal = col <= row
    strict = col < row
    gc_t = [jnp.transpose(gc[c * CHUNK:(c + 1) * CHUNK, :]) for c in range(nchunk)]

    for h in range(GDN_HEADS):
        lo = h * HEAD_DIM
        q_all = y[:, lo:lo + HEAD_DIM]
        k_all = y[:, GDN_W + lo:GDN_W + lo + HEAD_DIM]
        v_all = y[:, 2 * GDN_W + lo:2 * GDN_W + lo + HEAD_DIM]
        q_all = q_all * lax.rsqrt(jnp.sum(q_all * q_all, axis=-1, keepdims=True) + EPS) * HEAD_DIM ** -0.5
        k_all = k_all * lax.rsqrt(jnp.sum(k_all * k_all, axis=-1, keepdims=True) + EPS)
        state = s_ref[h]
        for c in range(nchunk):
            r0 = c * CHUNK
            q = q_all[r0:r0 + CHUNK]
            k = k_all[r0:r0 + CHUNK]
            v = v_all[r0:r0 + CHUNK]
            gcc = gc[r0:r0 + CHUNK, GATE_GA + h:GATE_GA + h + 1]
            gcr = gc_t[c][GATE_GA + h:GATE_GA + h + 1, :]
            bet = beta[r0:r0 + CHUNK, GATE_GB + h:GATE_GB + h + 1]
            g_last = gc[r0 + CHUNK - 1:r0 + CHUNK, GATE_GA + h:GATE_GA + h + 1]
            decay = jnp.exp(jnp.where(causal, gcc - gcr, -jnp.inf))
            kb = k * bet
            kq = _dot_nt(jnp.concatenate([kb, q], axis=0), k)
            a_mat = jnp.where(strict, kq[:CHUNK] * decay, 0.0)
            attn = kq[CHUNK:] * decay
            t_inv = _unit_lower_inverse(a_mat, row, col)
            eg = jnp.exp(gcc)
            uw = _dot(t_inv, jnp.concatenate([v * bet, kb * eg], axis=1))
            v_new = uw[:, :HEAD_DIM] - _dot(uw[:, HEAD_DIM:], state)
            o = _dot(jnp.concatenate([q * eg, attn], axis=1),
                     jnp.concatenate([state, v_new], axis=0))
            k_dec = k * jnp.exp(g_last - gcc)
            state = state * jnp.exp(g_last) + _dot_tn(k_dec, v_new)
            obuf_ref[r0:r0 + CHUNK, lo:lo + HEAD_DIM] = o
        s_ref[h] = state

    for h in range(GDN_HEADS):
        lo = h * HEAD_DIM
        o = obuf_ref[:, lo:lo + HEAD_DIM]
        o = o * lax.rsqrt(jnp.mean(o * o, axis=-1, keepdims=True) + EPS) * ng_ref[...]
        z = z_ref[:, lo:lo + HEAD_DIM].astype(F32)
        o_ref[:, lo:lo + HEAD_DIM] = (o * _silu(z)).astype(o_ref.dtype)


def _gdn(pg, gates, conv_w, alog_row, dtb_row, ng_row, bsz, seq, tq):
    nt = seq // tq
    row_map = lambda b, t: (b * nt + t, 0)
    full = lambda a: pl.BlockSpec(a.shape, lambda b, t: (0, 0))
    return pl.pallas_call(
        _gdn_kernel,
        grid=(bsz, nt),
        in_specs=[pl.BlockSpec((tq, 3 * GDN_W), row_map),
                  pl.BlockSpec((tq, GDN_W), lambda b, t: (b * nt + t, 3)),
                  pl.BlockSpec((tq, LANES), row_map),
                  full(conv_w), full(alog_row), full(dtb_row), full(ng_row)],
        out_specs=pl.BlockSpec((tq, GDN_W), row_map),
        out_shape=jax.ShapeDtypeStruct((bsz * seq, GDN_W), BF16),
        scratch_shapes=[pltpu.VMEM((tq + SUBLANES, 3 * GDN_W), F32),
                        pltpu.VMEM((GDN_HEADS, HEAD_DIM, HEAD_DIM), F32),
                        pltpu.VMEM((tq, GDN_W), F32)],
        compiler_params=pltpu.CompilerParams(dimension_semantics=("arbitrary", "arbitrary"),
                                             vmem_limit_bytes=VMEM_LIMIT),
        name="gdn",
    )(pg, pg, gates, conv_w, alog_row, dtb_row, ng_row)


def _mlstm_kernel(qkv_ref, og_ref, z_ref, gates_ref, ib_ref, fb_ref, ng_ref,
                  o_ref, c_ref, n_ref, m_ref, hbuf_ref):
    tq = qkv_ref.shape[0]
    nchunk = tq // CHUNK

    @pl.when(pl.program_id(1) == 0)
    def _():
        c_ref[...] = jnp.zeros(c_ref.shape, F32)
        n_ref[...] = jnp.zeros(n_ref.shape, F32)
        m_ref[...] = jnp.zeros(m_ref.shape, F32)

    gates = gates_ref[...]
    log_f = -_softplus(-(gates + fb_ref[...]))
    log_i = pltpu.roll(gates + ib_ref[...], GATE_MF - GATE_MI, 1)
    bcum = _chunk_cumsum(_chunk_tri(tq), log_f)
    rr = log_i - bcum

    row = lax.broadcasted_iota(jnp.int32, (CHUNK, CHUNK), 0)
    col = lax.broadcasted_iota(jnp.int32, (CHUNK, CHUNK), 1)
    causal = col <= row
    rr_t = [jnp.transpose(rr[c * CHUNK:(c + 1) * CHUNK, :]) for c in range(nchunk)]

    for h in range(MLSTM_HEADS):
        lo = h * HEAD_DIM
        ln = GATE_MF + h
        c_st = c_ref[h]
        n_st = n_ref[h:h + 1, :]
        m_st = m_ref[h:h + 1, 0:1]
        for c in range(nchunk):
            r0 = c * CHUNK
            q = qkv_ref[r0:r0 + CHUNK, lo:lo + HEAD_DIM].astype(F32)
            k = qkv_ref[r0:r0 + CHUNK, MLSTM_W + lo:MLSTM_W + lo + HEAD_DIM].astype(F32) * HEAD_DIM ** -0.5
            v = qkv_ref[r0:r0 + CHUNK, 2 * MLSTM_W + lo:2 * MLSTM_W + lo + HEAD_DIM].astype(F32)
            bc = bcum[r0:r0 + CHUNK, ln:ln + 1]
            rc = rr[r0:r0 + CHUNK, ln:ln + 1]
            rrow = rr_t[c][ln:ln + 1, :]
            b_last = bcum[r0 + CHUNK - 1:r0 + CHUNK, ln:ln + 1]
            log_w = jnp.where(causal, bc + rrow, -jnp.inf)
            m_intra = jnp.max(log_w, axis=-1, keepdims=True)
            m_t = jnp.maximum(bc + m_st, m_intra)
            pq = jnp.exp(log_w - m_t) * _dot_nt(q, k)
            a = jnp.exp(bc + m_st - m_t)
            num = _dot(jnp.concatenate([a * q, pq], axis=1), jnp.concatenate([c_st, v], axis=0))
            den = a * jnp.sum(q * n_st, axis=-1, keepdims=True) + jnp.sum(pq, axis=-1, keepdims=True)
            hbuf_ref[r0:r0 + CHUNK, lo:lo + HEAD_DIM] = num / jnp.maximum(jnp.abs(den), jnp.exp(-m_t))
            m_loc = b_last + jnp.max(rc, axis=0, keepdims=True)
            m_new = jnp.maximum(b_last + m_st, m_loc)
            s_old = jnp.exp(b_last + m_st - m_new)
            s_loc = jnp.exp(m_loc - m_new)
            k_end = k * jnp.exp(b_last + rc - m_loc)
            c_st = s_old * c_st + s_loc * _dot_tn(k_end, v)
            n_st = s_old * n_st + s_loc * jnp.sum(k_end, axis=0, keepdims=True)
            m_st = m_new
        c_ref[h] = c_st
        n_ref[h:h + 1, :] = n_st
        m_ref[h:h + 1, :] = jnp.broadcast_to(m_st, (1, LANES))

    for h in range(MLSTM_HEADS):
        lo = h * HEAD_DIM
        hh = hbuf_ref[:, lo:lo + HEAD_DIM] * _sigmoid(og_ref[:, lo:lo + HEAD_DIM].astype(F32))
        hh = hh * lax.rsqrt(jnp.mean(hh * hh, axis=-1, keepdims=True) + EPS) * ng_ref[h:h + 1, :]
        z = z_ref[:, lo:lo + HEAD_DIM].astype(F32)
        o_ref[:, lo:lo + HEAD_DIM] = (hh * _silu(z)).astype(o_ref.dtype)


def _mlstm(pm, gates, ib_row, fb_row, ng, bsz, seq, tq):
    nt = seq // tq
    row_map = lambda b, t: (b * nt + t, 0)
    full = lambda a: pl.BlockSpec(a.shape, lambda b, t: (0, 0))
    return pl.pallas_call(
        _mlstm_kernel,
        grid=(bsz, nt),
        in_specs=[pl.BlockSpec((tq, 3 * MLSTM_W), row_map),
                  pl.BlockSpec((tq, MLSTM_W), lambda b, t: (b * nt + t, 3)),
                  pl.BlockSpec((tq, MLSTM_W), lambda b, t: (b * nt + t, 4)),
                  pl.BlockSpec((tq, LANES), row_map),
                  full(ib_row), full(fb_row), full(ng)],
        out_specs=pl.BlockSpec((tq, MLSTM_W), row_map),
        out_shape=jax.ShapeDtypeStruct((bsz * seq, MLSTM_W), BF16),
        scratch_shapes=[pltpu.VMEM((MLSTM_HEADS, HEAD_DIM, HEAD_DIM), F32),
                        pltpu.VMEM((SUBLANES, HEAD_DIM), F32),
                        pltpu.VMEM((SUBLANES, LANES), F32),
                        pltpu.VMEM((tq, MLSTM_W), F32)],
        compiler_params=pltpu.CompilerParams(dimension_semantics=("arbitrary", "arbitrary"),
                                             vmem_limit_bytes=VMEM_LIMIT),
        name="mlstm",
    )(pm, pm, pm, gates, ib_row, fb_row, ng)


def _out_kernel(x_ref, og_ref, hm_ref, px_ref, kv_ref, w1_ref, w2_ref, w3_ref, fg_ref, o_ref):
    heads = []
    for h in range(MEM_HEADS):
        lo = h * HEAD_DIM
        xq = px_ref[:, lo:lo + HEAD_DIM]
        mk = kv_ref[0, :, lo:lo + HEAD_DIM]
        mv = kv_ref[0, :, MEM_W + lo:MEM_W + lo + HEAD_DIM]
        s = _dot_nt(xq, mk) * HEAD_DIM ** -0.5
        p = jnp.exp(s - jnp.max(s, axis=-1, keepdims=True))
        p = p / jnp.sum(p, axis=-1, keepdims=True)
        xz = px_ref[:, MEM_W + lo:MEM_W + lo + HEAD_DIM].astype(F32)
        heads.append(_dot(p, mv) * _silu(xz))
    o_mem = jnp.concatenate(heads, axis=1)
    y = (x_ref[...].astype(F32)
         + jnp.dot(og_ref[...], w1_ref[...], preferred_element_type=F32)
         + jnp.dot(hm_ref[...], w2_ref[...], preferred_element_type=F32)
         + _dot(o_mem, w3_ref[...]))
    o_ref[...] = (y * lax.rsqrt(jnp.mean(y * y, axis=-1, keepdims=True) + EPS) * fg_ref[...]).astype(o_ref.dtype)


def _outproj(x2, o_gdn, h_ml, px, kv, w_out, fg_row, bsz, seq, tm):
    nt = seq // tm
    d = x2.shape[1]
    row_map = lambda b, t: (b * nt + t, 0)
    return pl.pallas_call(
        _out_kernel,
        grid=(bsz, nt),
        in_specs=[pl.BlockSpec((tm, d), row_map),
                  pl.BlockSpec((tm, GDN_W), row_map),
                  pl.BlockSpec((tm, MLSTM_W), row_map),
                  pl.BlockSpec((tm, 2 * MEM_W), row_map),
                  pl.BlockSpec((1,) + kv.shape[1:], lambda b, t: (b, 0, 0)),
                  pl.BlockSpec((GDN_W, d), lambda b, t: (0, 0)),
                  pl.BlockSpec((MLSTM_W, d), lambda b, t: (1, 0)),
                  pl.BlockSpec((MEM_W, d), lambda b, t: ((GDN_W + MLSTM_W) // MEM_W, 0)),
                  pl.BlockSpec(fg_row.shape, lambda b, t: (0, 0))],
        out_specs=pl.BlockSpec((tm, d), row_map),
        out_shape=jax.ShapeDtypeStruct(x2.shape, x2.dtype),
        compiler_params=pltpu.CompilerParams(dimension_semantics=("arbitrary", "arbitrary"),
                                             vmem_limit_bytes=VMEM_LIMIT),
        name="outproj",
    )(x2, o_gdn, h_ml, px, kv, w_out, w_out, w_out, fg_row)


def _pad_row(v, start):
    return jnp.zeros((1, LANES), F32).at[0, start:start + v.shape[0]].set(v.astype(F32))


def kernel(x, mem, norm_g, w_in, conv_w, gdn_a_log, gdn_dt_bias, gdn_norm_g, mlstm_i_bias, mlstm_f_bias,
           mlstm_norm_g, mem_norm_g, w_mem_kv, w_out, final_norm_g):
    bsz, seq, d = x.shape
    depth = w_in.shape[0]
    assert depth == 1 and seq % 256 == 0
    x2 = x.reshape(bsz * seq, d)

    w = w_in[0]
    sizes = [GDN_W, GDN_W, GDN_W, GDN_HEADS, GDN_HEADS, GDN_W,
             MLSTM_W, MLSTM_W, MLSTM_W, MLSTM_HEADS, MLSTM_HEADS, MLSTM_W, MLSTM_W, MEM_W, MEM_W]
    offs = [0]
    for s in sizes:
        offs.append(offs[-1] + s)
    col = lambda i: w[:, offs[i]:offs[i + 1]]
    wg = jnp.concatenate([col(0), col(1), col(2), col(5)], axis=1).astype(BF16)
    wm = jnp.concatenate([col(6), col(7), col(8), col(11), col(12)], axis=1).astype(BF16)
    wx = jnp.concatenate([col(13), col(14)], axis=1).astype(BF16)
    wgate = jnp.concatenate([col(3), col(4), col(9), col(10)], axis=1).astype(F32)
    wgate = jnp.pad(wgate, ((0, 0), (0, LANES - wgate.shape[1])))

    pg, pm, px, gates = _inproj(x2, norm_g[0].reshape(1, d).astype(F32), wg, wm, wx, wgate, tm=256)
    kv = _memkv(mem, mem_norm_g[0].reshape(1, d).astype(F32), w_mem_kv[0].astype(BF16))

    o_gdn = _gdn(pg, gates, conv_w[0].astype(F32), _pad_row(gdn_a_log[0], GATE_GA),
                 _pad_row(gdn_dt_bias[0], GATE_GA), gdn_norm_g[0].reshape(1, HEAD_DIM).astype(F32),
                 bsz, seq, tq=256)
    h_ml = _mlstm(pm, gates, _pad_row(mlstm_i_bias[0], GATE_MI), _pad_row(mlstm_f_bias[0], GATE_MF),
                  jnp.pad(mlstm_norm_g[0].astype(F32), ((0, SUBLANES - MLSTM_HEADS), (0, 0))),
                  bsz, seq, tq=256)
    out = _outproj(x2, o_gdn, h_ml, px, kv, w_out[0].astype(BF16),
                   final_norm_g.reshape(1, d).astype(F32), bsz, seq, tm=256)
    return out.reshape(bsz, seq, d)
```

```python
import jax
import jax.numpy as jnp
from jax import lax
from jax.experimental import pallas as pl
from jax.experimental.pallas import tpu as pltpu

F32 = jnp.float32
BF16 = jnp.bfloat16

HEAD_DIM = 128
GDN_HEADS = 6
MLSTM_HEADS = 6
MEM_HEADS = 4
GDN_W = GDN_HEADS * HEAD_DIM
MLSTM_W = MLSTM_HEADS * HEAD_DIM
MEM_W = MEM_HEADS * HEAD_DIM
CONV_WIDTH = 4
CHUNK = 64
EPS = 1e-6
LANES = 128
SUBLANES = 8
VMEM_LIMIT = 56 * 1024 * 1024

GATE_GA = 0
GATE_GB = GDN_HEADS
GATE_MI = 2 * GDN_HEADS
GATE_MF = 2 * GDN_HEADS + MLSTM_HEADS


def _bdot(a, b):
    return jnp.dot(a, b, preferred_element_type=F32)


def _dot(a, b):
    return _bdot(a.astype(BF16), b.astype(BF16))


def _dot_nt(a, b):
    return lax.dot_general(a.astype(BF16), b.astype(BF16), (((1,), (1,)), ((), ())),
                           preferred_element_type=F32)


def _split2(x):
    hi = x.astype(BF16)
    lo = (x - hi.astype(F32)).astype(BF16)
    return hi, lo


def _sigmoid(x):
    return 1.0 / (1.0 + jnp.exp(-x))


def _silu(x):
    return x * _sigmoid(x)


def _softplus(x):
    return jnp.maximum(x, 0.0) + jnp.log1p(jnp.exp(-jnp.abs(x)))


def _chunk_tri(n):
    r = lax.broadcasted_iota(jnp.int32, (n, n), 0)
    c = lax.broadcasted_iota(jnp.int32, (n, n), 1)
    return jnp.where((r // CHUNK == c // CHUNK) & (c <= r), 1.0, 0.0).astype(BF16)


def _chunk_cumsum(tri, x):
    hi, lo = _split2(x)
    return _bdot(tri, hi) + _bdot(tri, lo)


def _lane_bcast(x, lane):
    return jnp.broadcast_to(x[:, lane:lane + 1], x.shape)


def _inproj_kernel(x_ref, ng_ref, wg_ref, wm_ref, wx_ref, wgate_ref,
                   pg_ref, pm_ref, px_ref, gates_ref):
    x = x_ref[...].astype(F32)
    hn = x * lax.rsqrt(jnp.mean(x * x, axis=-1, keepdims=True) + EPS) * ng_ref[...]
    hb = hn.astype(BF16)
    pg_ref[...] = _bdot(hb, wg_ref[...]).astype(pg_ref.dtype)
    pm_ref[...] = _bdot(hb, wm_ref[...]).astype(pm_ref.dtype)
    px_ref[...] = _bdot(hb, wx_ref[...]).astype(px_ref.dtype)
    h_lo = (hn - hb.astype(F32)).astype(BF16)
    w_hi, w_lo = _split2(wgate_ref[...])
    gates_ref[...] = _bdot(hb, w_hi) + _bdot(hb, w_lo) + _bdot(h_lo, w_hi)


def _inproj(x2, norm_g, wg, wm, wx, wgate, tm):
    m, d = x2.shape
    full = lambda a: pl.BlockSpec(a.shape, lambda i: (0, 0))
    return pl.pallas_call(
        _inproj_kernel,
        grid=(m // tm,),
        in_specs=[pl.BlockSpec((tm, d), lambda i: (i, 0)), full(norm_g), full(wg), full(wm), full(wx),
                  full(wgate)],
        out_specs=[pl.BlockSpec((tm, wg.shape[1]), lambda i: (i, 0)),
                   pl.BlockSpec((tm, wm.shape[1]), lambda i: (i, 0)),
                   pl.BlockSpec((tm, wx.shape[1]), lambda i: (i, 0)),
                   pl.BlockSpec((tm, LANES), lambda i: (i, 0))],
        out_shape=[jax.ShapeDtypeStruct((m, wg.shape[1]), BF16),
                   jax.ShapeDtypeStruct((m, wm.shape[1]), BF16),
                   jax.ShapeDtypeStruct((m, wx.shape[1]), BF16),
                   jax.ShapeDtypeStruct((m, LANES), F32)],
        compiler_params=pltpu.CompilerParams(dimension_semantics=("arbitrary",),
                                             vmem_limit_bytes=VMEM_LIMIT),
        name="inproj",
    )(x2, norm_g, wg, wm, wx, wgate)


def _memkv_kernel(mem_ref, g_ref, w_ref, kv_ref):
    x = mem_ref[0].astype(F32)
    xn = x * lax.rsqrt(jnp.mean(x * x, axis=-1, keepdims=True) + EPS) * g_ref[...]
    kv_ref[0] = _dot(xn, w_ref[...]).astype(kv_ref.dtype)


def _memkv(mem, g, w):
    b, n, d = mem.shape
    return pl.pallas_call(
        _memkv_kernel,
        grid=(b,),
        in_specs=[pl.BlockSpec((1, n, d), lambda i: (i, 0, 0)),
                  pl.BlockSpec(g.shape, lambda i: (0, 0)),
                  pl.BlockSpec(w.shape, lambda i: (0, 0))],
        out_specs=pl.BlockSpec((1, n, w.shape[1]), lambda i: (i, 0, 0)),
        out_shape=jax.ShapeDtypeStruct((b, n, w.shape[1]), BF16),
        compiler_params=pltpu.CompilerParams(dimension_semantics=("arbitrary",),
                                             vmem_limit_bytes=VMEM_LIMIT),
        name="memkv",
    )(mem, g, w)


def _unit_lower_inverses(a_list, row, col):
    eye = jnp.where(row == col, 1.0, 0.0)
    xs = [eye - jnp.where(row // 2 == col // 2, a, 0.0) for a in a_list]
    s = 2
    while s < CHUNK:
        lower_left = (row // (2 * s) == col // (2 * s)) & (row // s != col // s)
        xb = [x.astype(BF16) for x in xs]
        xa = [_bdot(x, jnp.where(lower_left, a, 0.0).astype(BF16)) for x, a in zip(xb, a_list)]
        xs = [x - _bdot(t.astype(BF16), b) for x, t, b in zip(xs, xa, xb)]
        s *= 2
    return xs


def _gdn_kernel(qkv_ref, z_ref, gates_ref, convw_ref, alog_ref, dtb_ref, ng_ref,
                o_ref, xpad_ref, s_ref, obuf_ref):
    nb, tq = qkv_ref.shape[0], qkv_ref.shape[1]
    nchunk = tq // CHUNK

    @pl.when(pl.program_id(1) == 0)
    def _():
        xpad_ref[:, 0:SUBLANES, :] = jnp.zeros((nb, SUBLANES, xpad_ref.shape[2]), F32)
        s_ref[...] = jnp.zeros(s_ref.shape, F32)

    row = lax.broadcasted_iota(jnp.int32, (CHUNK, CHUNK), 0)
    col = lax.broadcasted_iota(jnp.int32, (CHUNK, CHUNK), 1)
    causal = col <= row
    strict = col < row
    tri = _chunk_tri(tq)

    items = []
    kq_in = []
    for n in range(nb):
        xpad_ref[n, SUBLANES:SUBLANES + tq, :] = qkv_ref[n].astype(F32)
        acc = convw_ref[CONV_WIDTH - 1:CONV_WIDTH, :] * xpad_ref[n, SUBLANES:SUBLANES + tq, :]
        for j in range(CONV_WIDTH - 1):
            off = SUBLANES - (CONV_WIDTH - 1) + j
            acc = acc + convw_ref[j:j + 1, :] * xpad_ref[n, off:off + tq, :]
        xpad_ref[n, 0:SUBLANES, :] = xpad_ref[n, tq:tq + SUBLANES, :]
        y = _silu(acc)

        gates = gates_ref[n]
        g_log = -jnp.exp(alog_ref[...]) * _softplus(gates + dtb_ref[...])
        beta = _sigmoid(gates)
        gc = _chunk_cumsum(tri, g_log)
        gc_t = [jnp.transpose(gc[c * CHUNK:(c + 1) * CHUNK, :]) for c in range(nchunk)]

        for h in range(GDN_HEADS):
            lo = h * HEAD_DIM
            q_all = y[:, lo:lo + HEAD_DIM]
            k_all = y[:, GDN_W + lo:GDN_W + lo + HEAD_DIM]
            v_all = y[:, 2 * GDN_W + lo:2 * GDN_W + lo + HEAD_DIM]
            q_all = q_all * lax.rsqrt(jnp.sum(q_all * q_all, axis=-1, keepdims=True) + EPS) * HEAD_DIM ** -0.5
            k_all = k_all * lax.rsqrt(jnp.sum(k_all * k_all, axis=-1, keepdims=True) + EPS)
            bet = _lane_bcast(beta, GATE_GB + h)
            gcb = _lane_bcast(gc, GATE_GA + h)
            eg = jnp.exp(gcb)
            kb_all = k_all * bet
            rhs_all = jnp.concatenate([v_all * bet, kb_all * eg], axis=1).astype(BF16)
            qe_all = (q_all * eg).astype(BF16)
            kb_bf = kb_all.astype(BF16)
            q_bf = q_all.astype(BF16)
            k_bf = k_all.astype(BF16)
            for c in range(nchunk):
                r0 = c * CHUNK
                gcr = gc_t[c][GATE_GA + h:GATE_GA + h + 1, :]
                decay = jnp.exp(jnp.where(causal, gcb[r0:r0 + CHUNK, :CHUNK] - gcr, -jnp.inf))
                g_last = gcb[r0 + CHUNK - 1:r0 + CHUNK, :]
                k_dec = k_all[r0:r0 + CHUNK] * jnp.exp(g_last - gcb[r0:r0 + CHUNK])
                items.append(dict(n=n, h=h, c=c, decay=decay, rhs=rhs_all[r0:r0 + CHUNK],
                                  qe=qe_all[r0:r0 + CHUNK], k_dec=k_dec, eg_last=jnp.exp(g_last)))
                kq_in.append((jnp.concatenate([kb_bf[r0:r0 + CHUNK], q_bf[r0:r0 + CHUNK]], axis=0),
                              k_bf[r0:r0 + CHUNK]))

    kq = [lax.dot_general(kbq, k, (((1,), (1,)), ((), ())), preferred_element_type=F32) for kbq, k in kq_in]
    a_list = [jnp.where(strict, m[:CHUNK] * it["decay"], 0.0) for m, it in zip(kq, items)]
    attn = [(m[CHUNK:] * it["decay"]).astype(BF16) for m, it in zip(kq, items)]
    t_inv = _unit_lower_inverses(a_list, row, col)
    uw = [_bdot(x.astype(BF16), it["rhs"]) for x, it in zip(t_inv, items)]
    k_dec_t = [jnp.transpose(it["k_dec"]).astype(BF16) for it in items]

    groups = [(n, h) for n in range(nb) for h in range(GDN_HEADS)]
    index = {(it["n"], it["h"], it["c"]): i for i, it in enumerate(items)}
    state = [s_ref[n * GDN_HEADS + h] for n, h in groups]
    for c in range(nchunk):
        ids = [index[(n, h, c)] for n, h in groups]
        st_bf = [s.astype(BF16) for s in state]
        ws = [_bdot(uw[i][:, HEAD_DIM:].astype(BF16), sb) for i, sb in zip(ids, st_bf)]
        v_new = [(uw[i][:, :HEAD_DIM] - w).astype(BF16) for i, w in zip(ids, ws)]
        outs = [_bdot(jnp.concatenate([items[i]["qe"], attn[i]], axis=1), jnp.concatenate([sb, vn], axis=0))
                for i, sb, vn in zip(ids, st_bf, v_new)]
        state = [s * items[i]["eg_last"] + _bdot(k_dec_t[i], vn) for i, s, vn in zip(ids, state, v_new)]
        for (n, h), o in zip(groups, outs):
            obuf_ref[n, c * CHUNK:(c + 1) * CHUNK, h * HEAD_DIM:(h + 1) * HEAD_DIM] = o
    for (n, h), s in zip(groups, state):
        s_ref[n * GDN_HEADS + h] = s

    for n in range(nb):
        for h in range(GDN_HEADS):
            lo = h * HEAD_DIM
            o = obuf_ref[n, :, lo:lo + HEAD_DIM]
            o = o * lax.rsqrt(jnp.mean(o * o, axis=-1, keepdims=True) + EPS) * ng_ref[...]
            z = z_ref[n, :, lo:lo + HEAD_DIM].astype(F32)
            o_ref[n, :, lo:lo + HEAD_DIM] = (o * _silu(z)).astype(o_ref.dtype)


def _gdn(pg, gates, conv_w, alog_row, dtb_row, ng_row, nb, tq):
    bsz, seq, _ = pg.shape
    blk = lambda w, j: pl.BlockSpec((nb, tq, w), lambda b, t: (b, t, j))
    full = lambda a: pl.BlockSpec(a.shape, lambda b, t: (0, 0))
    return pl.pallas_call(
        _gdn_kernel,
        grid=(bsz // nb, seq // tq),
        in_specs=[blk(3 * GDN_W, 0), blk(GDN_W, 3), blk(LANES, 0),
                  full(conv_w), full(alog_row), full(dtb_row), full(ng_row)],
        out_specs=blk(GDN_W, 0),
        out_shape=jax.ShapeDtypeStruct((bsz, seq, GDN_W), BF16),
        scratch_shapes=[pltpu.VMEM((nb, tq + SUBLANES, 3 * GDN_W), F32),
                        pltpu.VMEM((nb * GDN_HEADS, HEAD_DIM, HEAD_DIM), F32),
                        pltpu.VMEM((nb, tq, GDN_W), F32)],
        compiler_params=pltpu.CompilerParams(dimension_semantics=("arbitrary", "arbitrary"),
                                             vmem_limit_bytes=VMEM_LIMIT),
        name="gdn",
    )(pg, pg, gates, conv_w, alog_row, dtb_row, ng_row)


def _mlstm_kernel(qkv_ref, og_ref, z_ref, gates_ref, ib_ref, fb_ref, ng_ref,
                  o_ref, c_ref, n_ref, m_ref, hbuf_ref):
    nb, tq = qkv_ref.shape[0], qkv_ref.shape[1]
    nchunk = tq // CHUNK

    @pl.when(pl.program_id(1) == 0)
    def _():
        c_ref[...] = jnp.zeros(c_ref.shape, F32)
        n_ref[...] = jnp.zeros(n_ref.shape, F32)
        m_ref[...] = jnp.zeros(m_ref.shape, F32)

    row = lax.broadcasted_iota(jnp.int32, (CHUNK, CHUNK), 0)
    col = lax.broadcasted_iota(jnp.int32, (CHUNK, CHUNK), 1)
    causal = col <= row
    tri = _chunk_tri(tq)

    pos = lax.broadcasted_iota(jnp.int32, (tq, LANES), 0) % CHUNK

    seqs = []
    for n in range(nb):
        gates = gates_ref[n]
        log_f = -_softplus(-(gates + fb_ref[...]))
        log_i = pltpu.roll(gates + ib_ref[...], GATE_MF - GATE_MI, 1)
        bcum = _chunk_cumsum(tri, log_f)
        rr = log_i - bcum
        rcm = rr
        s = 1
        while s < CHUNK:
            rcm = jnp.maximum(rcm, jnp.where(pos >= s, pltpu.roll(rcm, s, 0), -jnp.inf))
            s *= 2
        rr_t = [jnp.transpose(rr[c * CHUNK:(c + 1) * CHUNK, :]) for c in range(nchunk)]

        m_row = m_ref[n:n + 1, :]
        m_prev, blm, s_old, s_loc = [], [], [], []
        for c in range(nchunk):
            last = c * CHUNK + CHUNK - 1
            b_last = bcum[last:last + 1, :]
            m_loc = b_last + rcm[last:last + 1, :]
            m_new = jnp.maximum(b_last + m_row, m_loc)
            m_prev.append(m_row)
            blm.append(b_last - m_loc)
            s_old.append(jnp.exp(b_last + m_row - m_new))
            s_loc.append(jnp.exp(m_loc - m_new))
            m_row = m_new
        m_ref[n:n + 1, :] = m_row
        rows = [jnp.concatenate(r, axis=0) for r in (m_prev, blm, s_old, s_loc)]
        seqs.append((bcum, rr, rcm, rr_t, rows))

    items = []
    for n in range(nb):
        bcum, rr, rcm, rr_t, rows = seqs[n]
        for h in range(MLSTM_HEADS):
            lo = h * HEAD_DIM
            ln = GATE_MF + h
            bcb = _lane_bcast(bcum, ln)
            rrb = _lane_bcast(rr, ln)
            rcmb = _lane_bcast(rcm, ln)
            m_prev_h, blm_h, s_old_h, s_loc_h = [_lane_bcast(r, ln) for r in rows]
            for c in range(nchunk):
                r0 = c * CHUNK
                q = qkv_ref[n, r0:r0 + CHUNK, lo:lo + HEAD_DIM]
                k = qkv_ref[n, r0:r0 + CHUNK, MLSTM_W + lo:MLSTM_W + lo + HEAD_DIM]
                v = qkv_ref[n, r0:r0 + CHUNK, 2 * MLSTM_W + lo:2 * MLSTM_W + lo + HEAD_DIM]
                mm = jnp.maximum(m_prev_h[c:c + 1], rcmb[r0:r0 + CHUNK])
                p = jnp.exp(jnp.where(causal, rr_t[c][ln:ln + 1, :] - mm[:, :CHUNK], -jnp.inf))
                a = jnp.exp(m_prev_h[c:c + 1] - mm)
                k_end = k.astype(F32) * (HEAD_DIM ** -0.5) * jnp.exp(blm_h[c:c + 1] + rrb[r0:r0 + CHUNK])
                items.append(dict(n=n, h=h, c=c, q=q, k=k, v=v, p=p, a=a,
                                  inv_floor=jnp.exp(-(bcb[r0:r0 + CHUNK] + mm)), k_end=k_end,
                                  aq=(a * q.astype(F32)).astype(BF16),
                                  s_old=s_old_h[c:c + 1], s_loc=s_loc_h[c:c + 1]))

    k_end_t = [jnp.transpose(it["k_end"]).astype(BF16) for it in items]
    qk = [lax.dot_general(it["q"], it["k"], (((1,), (1,)), ((), ())), preferred_element_type=F32)
          for it in items]
    kv = [_bdot(kt, it["v"]) for kt, it in zip(k_end_t, items)]
    pq = [it["p"] * (m * HEAD_DIM ** -0.5) for m, it in zip(qk, items)]
    ksum = [jnp.sum(it["k_end"], axis=0, keepdims=True) for it in items]

    index = {(it["n"], it["h"], it["c"]): i for i, it in enumerate(items)}
    c_in, n_in = {}, {}
    for n in range(nb):
        for h in range(MLSTM_HEADS):
            g = n * MLSTM_HEADS + h
            c_st = c_ref[g]
            n_st = n_ref[g:g + 1, :]
            for c in range(nchunk):
                i = index[(n, h, c)]
                c_in[i], n_in[i] = c_st, n_st
                c_st = items[i]["s_old"] * c_st + items[i]["s_loc"] * kv[i]
                n_st = items[i]["s_old"] * n_st + items[i]["s_loc"] * ksum[i]
            c_ref[g] = c_st
            n_ref[g:g + 1, :] = n_st

    num = [_bdot(jnp.concatenate([it["aq"], pq[i].astype(BF16)], axis=1),
                 jnp.concatenate([c_in[i].astype(BF16), it["v"]], axis=0))
           for i, it in enumerate(items)]
    qn = [jnp.sum(it["q"].astype(F32) * n_in[i], axis=-1, keepdims=True) for i, it in enumerate(items)]
    psum = [jnp.sum(m, axis=-1, keepdims=True) for m in pq]
    for i, it in enumerate(items):
        den = it["a"] * qn[i] + psum[i]
        r0, lo = it["c"] * CHUNK, it["h"] * HEAD_DIM
        hbuf_ref[it["n"], r0:r0 + CHUNK, lo:lo + HEAD_DIM] = num[i] / jnp.maximum(jnp.abs(den), it["inv_floor"])

    for n in range(nb):
        for h in range(MLSTM_HEADS):
            lo = h * HEAD_DIM
            hh = hbuf_ref[n, :, lo:lo + HEAD_DIM] * _sigmoid(og_ref[n, :, lo:lo + HEAD_DIM].astype(F32))
            hh = hh * lax.rsqrt(jnp.mean(hh * hh, axis=-1, keepdims=True) + EPS) * ng_ref[h:h + 1, :]
            z = z_ref[n, :, lo:lo + HEAD_DIM].astype(F32)
            o_ref[n, :, lo:lo + HEAD_DIM] = (hh * _silu(z)).astype(o_ref.dtype)


def _mlstm(pm, gates, ib_row, fb_row, ng, nb, tq):
    bsz, seq, _ = pm.shape
    blk = lambda w, j: pl.BlockSpec((nb, tq, w), lambda b, t: (b, t, j))
    full = lambda a: pl.BlockSpec(a.shape, lambda b, t: (0, 0))
    return pl.pallas_call(
        _mlstm_kernel,
        grid=(bsz // nb, seq // tq),
        in_specs=[blk(3 * MLSTM_W, 0), blk(MLSTM_W, 3), blk(MLSTM_W, 4), blk(LANES, 0),
                  full(ib_row), full(fb_row), full(ng)],
        out_specs=blk(MLSTM_W, 0),
        out_shape=jax.ShapeDtypeStruct((bsz, seq, MLSTM_W), BF16),
        scratch_shapes=[pltpu.VMEM((nb * MLSTM_HEADS, HEAD_DIM, HEAD_DIM), F32),
                        pltpu.VMEM((max(nb * MLSTM_HEADS, SUBLANES), HEAD_DIM), F32),
                        pltpu.VMEM((max(nb, SUBLANES), LANES), F32),
                        pltpu.VMEM((nb, tq, MLSTM_W), F32)],
        compiler_params=pltpu.CompilerParams(dimension_semantics=("arbitrary", "arbitrary"),
                                             vmem_limit_bytes=VMEM_LIMIT),
        name="mlstm",
    )(pm, pm, pm, gates, ib_row, fb_row, ng)


def _out_kernel(x_ref, og_ref, hm_ref, px_ref, kv_ref, w1_ref, w2_ref, w3_ref, fg_ref, o_ref):
    heads = []
    for h in range(MEM_HEADS):
        lo = h * HEAD_DIM
        xq = px_ref[:, lo:lo + HEAD_DIM]
        mk = kv_ref[0, :, lo:lo + HEAD_DIM]
        mv = kv_ref[0, :, MEM_W + lo:MEM_W + lo + HEAD_DIM]
        s = _dot_nt(xq, mk) * HEAD_DIM ** -0.5
        p = jnp.exp(s - jnp.max(s, axis=-1, keepdims=True))
        p = p / jnp.sum(p, axis=-1, keepdims=True)
        xz = px_ref[:, MEM_W + lo:MEM_W + lo + HEAD_DIM].astype(F32)
        heads.append(_dot(p, mv) * _silu(xz))
    o_mem = jnp.concatenate(heads, axis=1)
    y = (x_ref[...].astype(F32)
         + _bdot(og_ref[...], w1_ref[...])
         + _bdot(hm_ref[...], w2_ref[...])
         + _dot(o_mem, w3_ref[...]))
    o_ref[...] = (y * lax.rsqrt(jnp.mean(y * y, axis=-1, keepdims=True) + EPS) * fg_ref[...]).astype(o_ref.dtype)


def _outproj(x2, o_gdn, h_ml, px, kv, w_out, fg_row, bsz, seq, tm):
    nt = seq // tm
    d = x2.shape[1]
    row_map = lambda b, t: (b * nt + t, 0)
    return pl.pallas_call(
        _out_kernel,
        grid=(bsz, nt),
        in_specs=[pl.BlockSpec((tm, d), row_map),
                  pl.BlockSpec((tm, GDN_W), row_map),
                  pl.BlockSpec((tm, MLSTM_W), row_map),
                  pl.BlockSpec((tm, 2 * MEM_W), row_map),
                  pl.BlockSpec((1,) + kv.shape[1:], lambda b, t: (b, 0, 0)),
                  pl.BlockSpec((GDN_W, d), lambda b, t: (0, 0)),
                  pl.BlockSpec((MLSTM_W, d), lambda b, t: (1, 0)),
                  pl.BlockSpec((MEM_W, d), lambda b, t: ((GDN_W + MLSTM_W) // MEM_W, 0)),
                  pl.BlockSpec(fg_row.shape, lambda b, t: (0, 0))],
        out_specs=pl.BlockSpec((tm, d), row_map),
        out_shape=jax.ShapeDtypeStruct(x2.shape, x2.dtype),
        compiler_params=pltpu.CompilerParams(dimension_semantics=("arbitrary", "arbitrary"),
                                             vmem_limit_bytes=VMEM_LIMIT),
        name="outproj",
    )(x2, o_gdn, h_ml, px, kv, w_out, w_out, w_out, fg_row)


def _pad_row(v, start):
    return jnp.zeros((1, LANES), F32).at[0, start:start + v.shape[0]].set(v.astype(F32))


def _tiling(bsz, seq):
    tm = 256
    tq = 256 if seq % 256 == 0 else CHUNK
    nb = 2 if bsz % 2 == 0 else 1
    return tm, nb, tq


def kernel(x, mem, norm_g, w_in, conv_w, gdn_a_log, gdn_dt_bias, gdn_norm_g, mlstm_i_bias, mlstm_f_bias,
           mlstm_norm_g, mem_norm_g, w_mem_kv, w_out, final_norm_g):
    bsz, seq, d = x.shape
    assert w_in.shape[0] == 1 and seq % 256 == 0
    tm, nb, tq = _tiling(bsz, seq)
    x2 = x.reshape(bsz * seq, d)

    w = w_in[0]
    sizes = [GDN_W, GDN_W, GDN_W, GDN_HEADS, GDN_HEADS, GDN_W,
             MLSTM_W, MLSTM_W, MLSTM_W, MLSTM_HEADS, MLSTM_HEADS, MLSTM_W, MLSTM_W, MEM_W, MEM_W]
    offs = [0]
    for s in sizes:
        offs.append(offs[-1] + s)
    col = lambda i: w[:, offs[i]:offs[i + 1]]
    wg = jnp.concatenate([col(0), col(1), col(2), col(5)], axis=1).astype(BF16)
    wm = jnp.concatenate([col(6), col(7), col(8), col(11), col(12)], axis=1).astype(BF16)
    wx = jnp.concatenate([col(13), col(14)], axis=1).astype(BF16)
    wgate = jnp.concatenate([col(3), col(4), col(9), col(10)], axis=1).astype(F32)
    wgate = jnp.pad(wgate, ((0, 0), (0, LANES - wgate.shape[1])))

    pg, pm, px, gates = _inproj(x2, norm_g[0].reshape(1, d).astype(F32), wg, wm, wx, wgate, tm)
    kv = _memkv(mem, mem_norm_g[0].reshape(1, d).astype(F32), w_mem_kv[0].astype(BF16))

    gates3 = gates.reshape(bsz, seq, LANES)
    o_gdn = _gdn(pg.reshape(bsz, seq, -1), gates3, conv_w[0].astype(F32), _pad_row(gdn_a_log[0], GATE_GA),
                 _pad_row(gdn_dt_bias[0], GATE_GA), gdn_norm_g[0].reshape(1, HEAD_DIM).astype(F32), nb, tq)
    h_ml = _mlstm(pm.reshape(bsz, seq, -1), gates3, _pad_row(mlstm_i_bias[0], GATE_MI),
                  _pad_row(mlstm_f_bias[0], GATE_MF),
                  jnp.pad(mlstm_norm_g[0].astype(F32), ((0, SUBLANES - MLSTM_HEADS), (0, 0))), nb, tq)
    out = _outproj(x2, o_gdn.reshape(bsz * seq, -1), h_ml.reshape(bsz * seq, -1), px, kv,
                   w_out[0].astype(BF16), final_norm_g.reshape(1, d).astype(F32), bsz, seq, tm)
    return out.reshape(bsz, seq, d)
```

```python
import functools

import jax
import jax.numpy as jnp
from jax import lax
from jax.experimental import pallas as pl
from jax.experimental.pallas import tpu as pltpu

F32 = jnp.float32
BF16 = jnp.bfloat16

HEAD_DIM = 128
GDN_HEADS = 6
MLSTM_HEADS = 6
MEM_HEADS = 4
GDN_W = GDN_HEADS * HEAD_DIM
MLSTM_W = MLSTM_HEADS * HEAD_DIM
MEM_W = MEM_HEADS * HEAD_DIM
CONV_WIDTH = 4
CHUNK = 64
EPS = 1e-6
LANES = 128
SUBLANES = 8
VMEM_LIMIT = 56 * 1024 * 1024

GATE_GA = 0
GATE_GB = GDN_HEADS
GATE_MI = 2 * GDN_HEADS
GATE_MF = 2 * GDN_HEADS + MLSTM_HEADS


def _bdot(a, b):
    return jnp.dot(a, b, preferred_element_type=F32)


def _dot(a, b):
    return _bdot(a.astype(BF16), b.astype(BF16))


def _dot_nt(a, b):
    return lax.dot_general(a.astype(BF16), b.astype(BF16), (((1,), (1,)), ((), ())),
                           preferred_element_type=F32)


def _split2(x):
    hi = x.astype(BF16)
    lo = (x - hi.astype(F32)).astype(BF16)
    return hi, lo


def _sigmoid(x):
    return 1.0 / (1.0 + jnp.exp(-x))


def _silu(x):
    return x * _sigmoid(x)


def _softplus(x):
    return jnp.maximum(x, 0.0) + jnp.log1p(jnp.exp(-jnp.abs(x)))


def _chunk_tri(n):
    r = lax.broadcasted_iota(jnp.int32, (n, n), 0)
    c = lax.broadcasted_iota(jnp.int32, (n, n), 1)
    return jnp.where((r // CHUNK == c // CHUNK) & (c <= r), 1.0, 0.0).astype(BF16)


def _chunk_cumsum(tri, x):
    hi, lo = _split2(x)
    return _bdot(tri, hi) + _bdot(tri, lo)


def _lane_bcast(x, lane):
    return jnp.broadcast_to(x[:, lane:lane + 1], x.shape)


def _inproj_kernel(x_ref, ng_ref, wg_ref, wm_ref, wx_ref, wgate_ref, convw_ref,
                   pg_ref, pm_ref, px_ref, gates_ref, xpad_ref, *, tiles_per_seq):
    tm = x_ref.shape[0]
    qkv_w = 3 * GDN_W

    @pl.when(pl.program_id(0) % tiles_per_seq == 0)
    def _():
        xpad_ref[0:SUBLANES, :] = jnp.zeros((SUBLANES, qkv_w), F32)

    x = x_ref[...].astype(F32)
    hn = x * lax.rsqrt(jnp.mean(x * x, axis=-1, keepdims=True) + EPS) * ng_ref[...]
    hb = hn.astype(BF16)

    g = _bdot(hb, wg_ref[...])
    xpad_ref[SUBLANES:SUBLANES + tm, :] = g[:, :qkv_w]
    acc = convw_ref[CONV_WIDTH - 1:CONV_WIDTH, :] * g[:, :qkv_w]
    for j in range(CONV_WIDTH - 1):
        off = SUBLANES - (CONV_WIDTH - 1) + j
        acc = acc + convw_ref[j:j + 1, :] * xpad_ref[off:off + tm, :]
    xpad_ref[0:SUBLANES, :] = xpad_ref[tm:tm + SUBLANES, :]
    y = _silu(acc)
    for h in range(GDN_HEADS):
        lo = h * HEAD_DIM
        q = y[:, lo:lo + HEAD_DIM]
        k = y[:, GDN_W + lo:GDN_W + lo + HEAD_DIM]
        q = q * (lax.rsqrt(jnp.sum(q * q, axis=-1, keepdims=True) + EPS) * HEAD_DIM ** -0.5)
        k = k * lax.rsqrt(jnp.sum(k * k, axis=-1, keepdims=True) + EPS)
        pg_ref[:, lo:lo + HEAD_DIM] = q.astype(BF16)
        pg_ref[:, GDN_W + lo:GDN_W + lo + HEAD_DIM] = k.astype(BF16)
    pg_ref[:, 2 * GDN_W:qkv_w] = y[:, 2 * GDN_W:].astype(BF16)
    pg_ref[:, qkv_w:] = _silu(g[:, qkv_w:]).astype(BF16)

    m = _bdot(hb, wm_ref[...])
    pm_ref[:, :3 * MLSTM_W] = m[:, :3 * MLSTM_W].astype(BF16)
    pm_ref[:, 3 * MLSTM_W:4 * MLSTM_W] = _sigmoid(m[:, 3 * MLSTM_W:4 * MLSTM_W]).astype(BF16)
    pm_ref[:, 4 * MLSTM_W:] = _silu(m[:, 4 * MLSTM_W:]).astype(BF16)

    xx = _bdot(hb, wx_ref[...])
    px_ref[:, :MEM_W] = xx[:, :MEM_W].astype(BF16)
    px_ref[:, MEM_W:] = _silu(xx[:, MEM_W:]).astype(BF16)

    h_lo = (hn - hb.astype(F32)).astype(BF16)
    w_hi, w_lo = _split2(wgate_ref[...])
    gates_ref[...] = _bdot(hb, w_hi) + _bdot(hb, w_lo) + _bdot(h_lo, w_hi)


def _inproj(x2, norm_g, wg, wm, wx, wgate, conv_w, tm, seq):
    m, d = x2.shape
    full = lambda a: pl.BlockSpec(a.shape, lambda i: (0, 0))
    return pl.pallas_call(
        functools.partial(_inproj_kernel, tiles_per_seq=seq // tm),
        grid=(m // tm,),
        in_specs=[pl.BlockSpec((tm, d), lambda i: (i, 0)), full(norm_g), full(wg), full(wm), full(wx),
                  full(wgate), full(conv_w)],
        scratch_shapes=[pltpu.VMEM((tm + SUBLANES, 3 * GDN_W), F32)],
        out_specs=[pl.BlockSpec((tm, wg.shape[1]), lambda i: (i, 0)),
                   pl.BlockSpec((tm, wm.shape[1]), lambda i: (i, 0)),
                   pl.BlockSpec((tm, wx.shape[1]), lambda i: (i, 0)),
                   pl.BlockSpec((tm, LANES), lambda i: (i, 0))],
        out_shape=[jax.ShapeDtypeStruct((m, wg.shape[1]), BF16),
                   jax.ShapeDtypeStruct((m, wm.shape[1]), BF16),
                   jax.ShapeDtypeStruct((m, wx.shape[1]), BF16),
                   jax.ShapeDtypeStruct((m, LANES), F32)],
        compiler_params=pltpu.CompilerParams(dimension_semantics=("arbitrary",),
                                             vmem_limit_bytes=VMEM_LIMIT),
        name="inproj",
    )(x2, norm_g, wg, wm, wx, wgate, conv_w)


def _memkv_kernel(mem_ref, g_ref, w_ref, kv_ref):
    x = mem_ref[0].astype(F32)
    xn = x * lax.rsqrt(jnp.mean(x * x, axis=-1, keepdims=True) + EPS) * g_ref[...]
    kv_ref[0] = _dot(xn, w_ref[...]).astype(kv_ref.dtype)


def _memkv(mem, g, w):
    b, n, d = mem.shape
    return pl.pallas_call(
        _memkv_kernel,
        grid=(b,),
        in_specs=[pl.BlockSpec((1, n, d), lambda i: (i, 0, 0)),
                  pl.BlockSpec(g.shape, lambda i: (0, 0)),
                  pl.BlockSpec(w.shape, lambda i: (0, 0))],
        out_specs=pl.BlockSpec((1, n, w.shape[1]), lambda i: (i, 0, 0)),
        out_shape=jax.ShapeDtypeStruct((b, n, w.shape[1]), BF16),
        compiler_params=pltpu.CompilerParams(dimension_semantics=("arbitrary",),
                                             vmem_limit_bytes=VMEM_LIMIT),
        name="memkv",
    )(mem, g, w)


def _unit_lower_inverses(a_list, row, col):
    eye = jnp.where(row == col, 1.0, 0.0)
    xs = [eye - jnp.where(row // 2 == col // 2, a, 0.0) for a in a_list]
    s = 2
    while s < CHUNK:
        lower_left = (row // (2 * s) == col // (2 * s)) & (row // s != col // s)
        xb = [x.astype(BF16) for x in xs]
        xa = [_bdot(x, jnp.where(lower_left, a, 0.0).astype(BF16)) for x, a in zip(xb, a_list)]
        xs = [x - _bdot(t.astype(BF16), b) for x, t, b in zip(xs, xa, xb)]
        s *= 2
    return xs


def _gdn_kernel(qkv_ref, gates_ref, alog_ref, dtb_ref, o_ref, s_ref):
    nb, tq = qkv_ref.shape[0], qkv_ref.shape[1]
    nchunk = tq // CHUNK

    @pl.when(pl.program_id(1) == 0)
    def _():
        s_ref[...] = jnp.zeros(s_ref.shape, F32)

    row = lax.broadcasted_iota(jnp.int32, (CHUNK, CHUNK), 0)
    col = lax.broadcasted_iota(jnp.int32, (CHUNK, CHUNK), 1)
    causal = col <= row
    strict = col < row
    tri = _chunk_tri(tq)

    items = []
    kq_in = []
    for n in range(nb):
        gates = gates_ref[n]
        g_log = -jnp.exp(alog_ref[...]) * _softplus(gates + dtb_ref[...])
        beta = _sigmoid(gates)
        gc = _chunk_cumsum(tri, g_log)
        gc_t = [jnp.transpose(gc[c * CHUNK:(c + 1) * CHUNK, :]) for c in range(nchunk)]

        for h in range(GDN_HEADS):
            lo = h * HEAD_DIM
            q_bf = qkv_ref[n, :, lo:lo + HEAD_DIM]
            k_bf = qkv_ref[n, :, GDN_W + lo:GDN_W + lo + HEAD_DIM]
            q_all = q_bf.astype(F32)
            k_all = k_bf.astype(F32)
            v_all = qkv_ref[n, :, 2 * GDN_W + lo:2 * GDN_W + lo + HEAD_DIM].astype(F32)
            bet = _lane_bcast(beta, GATE_GB + h)
            gcb = _lane_bcast(gc, GATE_GA + h)
            eg = jnp.exp(gcb)
            kb_all = k_all * bet
            rhs_all = jnp.concatenate([v_all * bet, kb_all * eg], axis=1).astype(BF16)
            qe_all = (q_all * eg).astype(BF16)
            kb_bf = kb_all.astype(BF16)
            for c in range(nchunk):
                r0 = c * CHUNK
                gcr = gc_t[c][GATE_GA + h:GATE_GA + h + 1, :]
                decay = jnp.exp(jnp.where(causal, gcb[r0:r0 + CHUNK, :CHUNK] - gcr, -jnp.inf))
                g_last = gcb[r0 + CHUNK - 1:r0 + CHUNK, :]
                k_dec = k_all[r0:r0 + CHUNK] * jnp.exp(g_last - gcb[r0:r0 + CHUNK])
                items.append(dict(n=n, h=h, c=c, decay=decay, rhs=rhs_all[r0:r0 + CHUNK],
                                  qe=qe_all[r0:r0 + CHUNK], k_dec=k_dec, eg_last=jnp.exp(g_last)))
                kq_in.append((jnp.concatenate([kb_bf[r0:r0 + CHUNK], q_bf[r0:r0 + CHUNK]], axis=0),
                              k_bf[r0:r0 + CHUNK]))

    kq = [lax.dot_general(kbq, k, (((1,), (1,)), ((), ())), preferred_element_type=F32) for kbq, k in kq_in]
    a_list = [jnp.where(strict, m[:CHUNK] * it["decay"], 0.0) for m, it in zip(kq, items)]
    attn = [(m[CHUNK:] * it["decay"]).astype(BF16) for m, it in zip(kq, items)]
    t_inv = _unit_lower_inverses(a_list, row, col)
    uw = [_bdot(x.astype(BF16), it["rhs"]) for x, it in zip(t_inv, items)]
    k_dec_t = [jnp.transpose(it["k_dec"]).astype(BF16) for it in items]

    groups = [(n, h) for n in range(nb) for h in range(GDN_HEADS)]
    index = {(it["n"], it["h"], it["c"]): i for i, it in enumerate(items)}
    state = [s_ref[n * GDN_HEADS + h] for n, h in groups]
    for c in range(nchunk):
        ids = [index[(n, h, c)] for n, h in groups]
        st_bf = [s.astype(BF16) for s in state]
        ws = [_bdot(uw[i][:, HEAD_DIM:].astype(BF16), sb) for i, sb in zip(ids, st_bf)]
        v_new = [(uw[i][:, :HEAD_DIM] - w).astype(BF16) for i, w in zip(ids, ws)]
        outs = [_bdot(jnp.concatenate([items[i]["qe"], attn[i]], axis=1), jnp.concatenate([sb, vn], axis=0))
                for i, sb, vn in zip(ids, st_bf, v_new)]
        state = [s * items[i]["eg_last"] + _bdot(k_dec_t[i], vn) for i, s, vn in zip(ids, state, v_new)]
        for (n, h), o in zip(groups, outs):
            o_ref[n, c * CHUNK:(c + 1) * CHUNK, h * HEAD_DIM:(h + 1) * HEAD_DIM] = o.astype(o_ref.dtype)
    for (n, h), s in zip(groups, state):
        s_ref[n * GDN_HEADS + h] = s


def _gdn(pg, gates, alog_row, dtb_row, nb, tq):
    bsz, seq, _ = pg.shape
    blk = lambda w, j: pl.BlockSpec((nb, tq, w), lambda b, t: (b, t, j))
    full = lambda a: pl.BlockSpec(a.shape, lambda b, t: (0, 0))
    return pl.pallas_call(
        _gdn_kernel,
        grid=(bsz // nb, seq // tq),
        in_specs=[blk(3 * GDN_W, 0), blk(LANES, 0), full(alog_row), full(dtb_row)],
        out_specs=blk(GDN_W, 0),
        out_shape=jax.ShapeDtypeStruct((bsz, seq, GDN_W), BF16),
        scratch_shapes=[pltpu.VMEM((nb * GDN_HEADS, HEAD_DIM, HEAD_DIM), F32)],
        compiler_params=pltpu.CompilerParams(dimension_semantics=("arbitrary", "arbitrary"),
                                             vmem_limit_bytes=VMEM_LIMIT),
        name="gdn",
    )(pg, gates, alog_row, dtb_row)


def _mlstm_kernel(qkv_ref, gates_ref, ib_ref, fb_ref, o_ref, c_ref, n_ref, m_ref):
    nb, tq = qkv_ref.shape[0], qkv_ref.shape[1]
    nchunk = tq // CHUNK

    @pl.when(pl.program_id(1) == 0)
    def _():
        c_ref[...] = jnp.zeros(c_ref.shape, F32)
        n_ref[...] = jnp.zeros(n_ref.shape, F32)
        m_ref[...] = jnp.zeros(m_ref.shape, F32)

    row = lax.broadcasted_iota(jnp.int32, (CHUNK, CHUNK), 0)
    col = lax.broadcasted_iota(jnp.int32, (CHUNK, CHUNK), 1)
    causal = col <= row
    tri = _chunk_tri(tq)

    pos = lax.broadcasted_iota(jnp.int32, (tq, LANES), 0) % CHUNK

    seqs = []
    for n in range(nb):
        gates = gates_ref[n]
        log_f = -_softplus(-(gates + fb_ref[...]))
        log_i = pltpu.roll(gates + ib_ref[...], GATE_MF - GATE_MI, 1)
        bcum = _chunk_cumsum(tri, log_f)
        rr = log_i - bcum
        rcm = rr
        s = 1
        while s < CHUNK:
            rcm = jnp.maximum(rcm, jnp.where(pos >= s, pltpu.roll(rcm, s, 0), -jnp.inf))
            s *= 2
        rr_t = [jnp.transpose(rr[c * CHUNK:(c + 1) * CHUNK, :]) for c in range(nchunk)]

        m_row = m_ref[n:n + 1, :]
        m_prev, blm, s_old, s_loc = [], [], [], []
        for c in range(nchunk):
            last = c * CHUNK + CHUNK - 1
            b_last = bcum[last:last + 1, :]
            m_loc = b_last + rcm[last:last + 1, :]
            m_new = jnp.maximum(b_last + m_row, m_loc)
            m_prev.append(m_row)
            blm.append(b_last - m_loc)
            s_old.append(jnp.exp(b_last + m_row - m_new))
            s_loc.append(jnp.exp(m_loc - m_new))
            m_row = m_new
        m_ref[n:n + 1, :] = m_row
        rows = [jnp.concatenate(r, axis=0) for r in (m_prev, blm, s_old, s_loc)]
        seqs.append((bcum, rr, rcm, rr_t, rows))

    items = []
    for n in range(nb):
        bcum, rr, rcm, rr_t, rows = seqs[n]
        for h in range(MLSTM_HEADS):
            lo = h * HEAD_DIM
            ln = GATE_MF + h
            bcb = _lane_bcast(bcum, ln)
            rrb = _lane_bcast(rr, ln)
            rcmb = _lane_bcast(rcm, ln)
            m_prev_h, blm_h, s_old_h, s_loc_h = [_lane_bcast(r, ln) for r in rows]
            for c in range(nchunk):
                r0 = c * CHUNK
                q = qkv_ref[n, r0:r0 + CHUNK, lo:lo + HEAD_DIM]
                k = qkv_ref[n, r0:r0 + CHUNK, MLSTM_W + lo:MLSTM_W + lo + HEAD_DIM]
                v = qkv_ref[n, r0:r0 + CHUNK, 2 * MLSTM_W + lo:2 * MLSTM_W + lo + HEAD_DIM]
                mm = jnp.maximum(m_prev_h[c:c + 1], rcmb[r0:r0 + CHUNK])
                p = jnp.exp(jnp.where(causal, rr_t[c][ln:ln + 1, :] - mm[:, :CHUNK], -jnp.inf))
                a = jnp.exp(m_prev_h[c:c + 1] - mm)
                k_end = k.astype(F32) * (HEAD_DIM ** -0.5) * jnp.exp(blm_h[c:c + 1] + rrb[r0:r0 + CHUNK])
                items.append(dict(n=n, h=h, c=c, q=q, k=k, v=v, p=p, a=a,
                                  inv_floor=jnp.exp(-(bcb[r0:r0 + CHUNK] + mm)), k_end=k_end,
                                  aq=(a * q.astype(F32)).astype(BF16),
                                  s_old=s_old_h[c:c + 1], s_loc=s_loc_h[c:c + 1]))

    k_end_t = [jnp.transpose(it["k_end"]).astype(BF16) for it in items]
    qk = [lax.dot_general(it["q"], it["k"], (((1,), (1,)), ((), ())), preferred_element_type=F32)
          for it in items]
    kv = [_bdot(kt, it["v"]) for kt, it in zip(k_end_t, items)]
    pq = [it["p"] * (m * HEAD_DIM ** -0.5) for m, it in zip(qk, items)]
    ksum = [jnp.sum(it["k_end"], axis=0, keepdims=True) for it in items]

    index = {(it["n"], it["h"], it["c"]): i for i, it in enumerate(items)}
    c_in, n_in = {}, {}
    for n in range(nb):
        for h in range(MLSTM_HEADS):
            g = n * MLSTM_HEADS + h
            c_st = c_ref[g]
            n_st = n_ref[g:g + 1, :]
            for c in range(nchunk):
                i = index[(n, h, c)]
                c_in[i], n_in[i] = c_st, n_st
                c_st = items[i]["s_old"] * c_st + items[i]["s_loc"] * kv[i]
                n_st = items[i]["s_old"] * n_st + items[i]["s_loc"] * ksum[i]
            c_ref[g] = c_st
            n_ref[g:g + 1, :] = n_st

    num = [_bdot(jnp.concatenate([it["aq"], pq[i].astype(BF16)], axis=1),
                 jnp.concatenate([c_in[i].astype(BF16), it["v"]], axis=0))
           for i, it in enumerate(items)]
    qn = [jnp.sum(it["q"].astype(F32) * n_in[i], axis=-1, keepdims=True) for i, it in enumerate(items)]
    psum = [jnp.sum(m, axis=-1, keepdims=True) for m in pq]
    for i, it in enumerate(items):
        den = it["a"] * qn[i] + psum[i]
        r0, lo = it["c"] * CHUNK, it["h"] * HEAD_DIM
        hid = num[i] / jnp.maximum(jnp.abs(den), it["inv_floor"])
        o_ref[it["n"], r0:r0 + CHUNK, lo:lo + HEAD_DIM] = hid.astype(o_ref.dtype)


def _mlstm(pm, gates, ib_row, fb_row, nb, tq):
    bsz, seq, _ = pm.shape
    blk = lambda w, j: pl.BlockSpec((nb, tq, w), lambda b, t: (b, t, j))
    full = lambda a: pl.BlockSpec(a.shape, lambda b, t: (0, 0))
    return pl.pallas_call(
        _mlstm_kernel,
        grid=(bsz // nb, seq // tq),
        in_specs=[blk(3 * MLSTM_W, 0), blk(LANES, 0), full(ib_row), full(fb_row)],
        out_specs=blk(MLSTM_W, 0),
        out_shape=jax.ShapeDtypeStruct((bsz, seq, MLSTM_W), BF16),
        scratch_shapes=[pltpu.VMEM((nb * MLSTM_HEADS, HEAD_DIM, HEAD_DIM), F32),
                        pltpu.VMEM((max(nb * MLSTM_HEADS, SUBLANES), HEAD_DIM), F32),
                        pltpu.VMEM((max(nb, SUBLANES), LANES), F32)],
        compiler_params=pltpu.CompilerParams(dimension_semantics=("arbitrary", "arbitrary"),
                                             vmem_limit_bytes=VMEM_LIMIT),
        name="mlstm",
    )(pm, gates, ib_row, fb_row)


def _out_kernel(x_ref, og_ref, zg_ref, hm_ref, om_ref, zm_ref, px_ref, kv_ref, w1_ref, w2_ref, w3_ref,
                gng_ref, mng_ref, fg_ref, o_ref):
    scores = [_dot_nt(px_ref[:, h * HEAD_DIM:(h + 1) * HEAD_DIM], kv_ref[0, :, h * HEAD_DIM:(h + 1) * HEAD_DIM])
              for h in range(MEM_HEADS)]

    parts = []
    for h in range(GDN_HEADS):
        lo = h * HEAD_DIM
        o = og_ref[:, lo:lo + HEAD_DIM].astype(F32)
        o = o * lax.rsqrt(jnp.mean(o * o, axis=-1, keepdims=True) + EPS) * gng_ref[...]
        parts.append((o * zg_ref[:, lo:lo + HEAD_DIM].astype(F32)).astype(BF16))
    y = x_ref[...].astype(F32) + _bdot(jnp.concatenate(parts, axis=1), w1_ref[...])

    parts = []
    for h in range(MLSTM_HEADS):
        lo = h * HEAD_DIM
        hh = hm_ref[:, lo:lo + HEAD_DIM].astype(F32) * om_ref[:, lo:lo + HEAD_DIM].astype(F32)
        hh = hh * lax.rsqrt(jnp.mean(hh * hh, axis=-1, keepdims=True) + EPS) * mng_ref[h:h + 1, :]
        parts.append((hh * zm_ref[:, lo:lo + HEAD_DIM].astype(F32)).astype(BF16))
    y = y + _bdot(jnp.concatenate(parts, axis=1), w2_ref[...])

    probs = []
    for s in scores:
        s = s * HEAD_DIM ** -0.5
        p = jnp.exp(s - jnp.max(s, axis=-1, keepdims=True))
        probs.append((p / jnp.sum(p, axis=-1, keepdims=True)).astype(BF16))
    parts = []
    for h in range(MEM_HEADS):
        lo = h * HEAD_DIM
        pv = _bdot(probs[h], kv_ref[0, :, MEM_W + lo:MEM_W + lo + HEAD_DIM])
        parts.append((pv * px_ref[:, MEM_W + lo:MEM_W + lo + HEAD_DIM].astype(F32)).astype(BF16))
    y = y + _bdot(jnp.concatenate(parts, axis=1), w3_ref[...])

    o_ref[...] = (y * lax.rsqrt(jnp.mean(y * y, axis=-1, keepdims=True) + EPS) * fg_ref[...]).astype(o_ref.dtype)


def _outproj(x2, o_gdn, h_ml, pg, pm, px, kv, w_out, gng_row, mng, fg_row, bsz, seq, tm):
    nt = seq // tm
    d = x2.shape[1]
    row_map = lambda b, t: (b * nt + t, 0)
    col_blk = lambda w, j: pl.BlockSpec((tm, w), lambda b, t: (b * nt + t, j))
    full = lambda a: pl.BlockSpec(a.shape, lambda b, t: (0, 0))
    return pl.pallas_call(
        _out_kernel,
        grid=(bsz, nt),
        in_specs=[pl.BlockSpec((tm, d), row_map),
                  col_blk(GDN_W, 0), col_blk(GDN_W, 3),
                  col_blk(MLSTM_W, 0), col_blk(MLSTM_W, 3), col_blk(MLSTM_W, 4),
                  col_blk(2 * MEM_W, 0),
                  pl.BlockSpec((1,) + kv.shape[1:], lambda b, t: (b, 0, 0)),
                  pl.BlockSpec((GDN_W, d), lambda b, t: (0, 0)),
                  pl.BlockSpec((MLSTM_W, d), lambda b, t: (1, 0)),
                  pl.BlockSpec((MEM_W, d), lambda b, t: ((GDN_W + MLSTM_W) // MEM_W, 0)),
                  full(gng_row), full(mng), full(fg_row)],
        out_specs=pl.BlockSpec((tm, d), row_map),
        out_shape=jax.ShapeDtypeStruct(x2.shape, x2.dtype),
        compiler_params=pltpu.CompilerParams(dimension_semantics=("arbitrary", "arbitrary"),
                                             vmem_limit_bytes=VMEM_LIMIT),
        name="outproj",
    )(x2, o_gdn, pg, h_ml, pm, pm, px, kv, w_out, w_out, w_out, gng_row, mng, fg_row)


def _pad_row(v, start):
    return jnp.zeros((1, LANES), F32).at[0, start:start + v.shape[0]].set(v.astype(F32))


def _tiling(bsz, seq):
    tm = 256
    tq = 256 if seq % 256 == 0 else CHUNK
    nb = 2 if bsz % 2 == 0 else 1
    return tm, nb, tq


def kernel(x, mem, norm_g, w_in, conv_w, gdn_a_log, gdn_dt_bias, gdn_norm_g, mlstm_i_bias, mlstm_f_bias,
           mlstm_norm_g, mem_norm_g, w_mem_kv, w_out, final_norm_g):
    bsz, seq, d = x.shape
    assert w_in.shape[0] == 1 and seq % 256 == 0
    tm, nb, tq = _tiling(bsz, seq)
    x2 = x.reshape(bsz * seq, d)

    w = w_in[0]
    sizes = [GDN_W, GDN_W, GDN_W, GDN_HEADS, GDN_HEADS, GDN_W,
             MLSTM_W, MLSTM_W, MLSTM_W, MLSTM_HEADS, MLSTM_HEADS, MLSTM_W, MLSTM_W, MEM_W, MEM_W]
    offs = [0]
    for s in sizes:
        offs.append(offs[-1] + s)
    col = lambda i: w[:, offs[i]:offs[i + 1]]
    wg = jnp.concatenate([col(0), col(1), col(2), col(5)], axis=1).astype(BF16)
    wm = jnp.concatenate([col(6), col(7), col(8), col(11), col(12)], axis=1).astype(BF16)
    wx = jnp.concatenate([col(13), col(14)], axis=1).astype(BF16)
    wgate = jnp.concatenate([col(3), col(4), col(9), col(10)], axis=1).astype(F32)
    wgate = jnp.pad(wgate, ((0, 0), (0, LANES - wgate.shape[1])))

    pg, pm, px, gates = _inproj(x2, norm_g[0].reshape(1, d).astype(F32), wg, wm, wx, wgate,
                                conv_w[0].astype(F32), tm, seq)
    kv = _memkv(mem, mem_norm_g[0].reshape(1, d).astype(F32), w_mem_kv[0].astype(BF16))

    gates3 = gates.reshape(bsz, seq, LANES)
    o_gdn = _gdn(pg.reshape(bsz, seq, -1), gates3, _pad_row(gdn_a_log[0], GATE_GA),
                 _pad_row(gdn_dt_bias[0], GATE_GA), nb, tq)
    h_ml = _mlstm(pm.reshape(bsz, seq, -1), gates3, _pad_row(mlstm_i_bias[0], GATE_MI),
                  _pad_row(mlstm_f_bias[0], GATE_MF), nb, tq)
    out = _outproj(x2, o_gdn.reshape(bsz * seq, -1), h_ml.reshape(bsz * seq, -1), pg, pm, px, kv,
                   w_out[0].astype(BF16), gdn_norm_g[0].reshape(1, HEAD_DIM).astype(F32),
                   jnp.pad(mlstm_norm_g[0].astype(F32), ((0, SUBLANES - MLSTM_HEADS), (0, 0))),
                   final_norm_g.reshape(1, d).astype(F32), bsz, seq, tm)
    return out.reshape(bsz, seq, d)
```
